```python
import jax, jax.numpy as jnp
from jax import lax
import numpy as np

D_MODEL = 1024
BATCH = 2
SEQ = 8192
DEPTH = 4

N_BRANCH = 4
BRANCH_WIDTH = D_MODEL // 2
CHUNK = 128
SG_GROUPS = 4
SG_GROUP_DIM = BRANCH_WIDTH // SG_GROUPS
CONV_WIDTH = 3
MLA_HEADS = 8
MLA_NOPE_DIM = 64
MLA_ROPE_DIM = 32
MLA_V_DIM = BRANCH_WIDTH // MLA_HEADS
MLA_Q_RANK = 256
MLA_KV_RANK = 128
ROPE_THETA = 10000.0
SB_HEADS = 8
SB_HEAD_DIM = BRANCH_WIDTH // SB_HEADS
Q_BLOCK = 128
D_FF = 4 * D_MODEL
NORM_EPS = 1e-6

COLS_SG = 2 * BRANCH_WIDTH
COLS_CONV = 3 * BRANCH_WIDTH
COLS_MLA = MLA_Q_RANK + MLA_KV_RANK + MLA_ROPE_DIM
COLS_SB = 3 * SB_HEADS * SB_HEAD_DIM
COLS_GATE = N_BRANCH * D_MODEL
SPLIT_POINTS = (COLS_SG, COLS_SG + COLS_CONV, COLS_SG + COLS_CONV + COLS_MLA,
                COLS_SG + COLS_CONV + COLS_MLA + COLS_SB)
IN_COLS = COLS_SG + COLS_CONV + COLS_MLA + COLS_SB + COLS_GATE

kernel_name = 'hybrid_gated_parallel_mixer_trunk'


def _rmsnorm(x, g):
    x32 = x.astype(jnp.float32)
    y = x32 * lax.rsqrt(jnp.mean(x32 * x32, axis=-1, keepdims=True) + NORM_EPS)
    return y.astype(x.dtype) * g


def _layernorm(x, g):
    x32 = x.astype(jnp.float32)
    xc = x32 - jnp.mean(x32, axis=-1, keepdims=True)
    y = xc * lax.rsqrt(jnp.mean(xc * xc, axis=-1, keepdims=True) + NORM_EPS)
    return y.astype(x.dtype) * g


def _rope(x, cos, sin):
    x1, x2 = jnp.split(x, 2, axis=-1)
    return jnp.concatenate([x1 * cos - x2 * sin, x2 * cos + x1 * sin], axis=-1)


def _to_blocks(t):
    b, s, h, d = t.shape
    return t.reshape(b, s // Q_BLOCK, Q_BLOCK, h, d).transpose(1, 0, 2, 3, 4)


def _from_blocks(o):
    nb, b, qb, h, d = o.shape
    return o.transpose(1, 0, 2, 3, 4).reshape(b, nb * qb, h * d)


def _spatial_gating(z, norm_g, w_s, b_s):
    bsz, seq, _ = z.shape
    u, v = jnp.split(jax.nn.gelu(z), 2, axis=-1)
    v = _layernorm(v, norm_g).reshape(bsz, seq // CHUNK, CHUNK, SG_GROUPS, SG_GROUP_DIM)
    causal = jnp.tril(jnp.ones((CHUNK, CHUNK), dtype=w_s.dtype))
    s = jnp.einsum('gts,bnsgc->bntgc', w_s * causal, v) + b_s.T[:, :, None]
    return u * s.reshape(bsz, seq, BRANCH_WIDTH)


def _short_conv(z, conv_w):
    gate_b, gate_c, xv = jnp.split(z, 3, axis=-1)
    t = gate_c * xv
    y = lax.conv_general_dilated(t, conv_w[:, None, :], window_strides=(1,),
                                 padding=[(CONV_WIDTH - 1, 0)],
                                 dimension_numbers=('NWC', 'WIO', 'NWC'),
                                 feature_group_count=BRANCH_WIDTH)
    return gate_b * y


def _latent_attention(z, cos, sin, q_norm_g, w_uq, kv_norm_g, w_ukv):
    bsz, seq, _ = z.shape
    c_q, c_kv, k_pe = jnp.split(z, [MLA_Q_RANK, MLA_Q_RANK + MLA_KV_RANK], axis=-1)
    q = (_rmsnorm(c_q, q_norm_g) @ w_uq).reshape(bsz, seq, MLA_HEADS, MLA_NOPE_DIM + MLA_ROPE_DIM)
    q_nope, q_pe = jnp.split(q, [MLA_NOPE_DIM], axis=-1)
    q_pe = _rope(q_pe, cos[:, :, None, :], sin[:, :, None, :])
    kv = (_rmsnorm(c_kv, kv_norm_g) @ w_ukv).reshape(bsz, seq, MLA_HEADS, MLA_NOPE_DIM + MLA_V_DIM)
    k_nope, v = jnp.split(kv, [MLA_NOPE_DIM], axis=-1)
    k_pe = _rope(k_pe, cos, sin)
    scale = (MLA_NOPE_DIM + MLA_ROPE_DIM) ** -0.5
    k_pos = jnp.arange(seq)

    def block(args):
        i, qn, qp = args
        q_pos = i * Q_BLOCK + jnp.arange(Q_BLOCK)
        s = (jnp.einsum('bqhd,bkhd->bhqk', qn, k_nope, preferred_element_type=jnp.float32)
             + jnp.einsum('bqhr,bkr->bhqk', qp, k_pe, preferred_element_type=jnp.float32)) * scale
        s = jnp.where(k_pos[None, :] <= q_pos[:, None], s, -jnp.inf)
        p = jax.nn.softmax(s, axis=-1).astype(v.dtype)
        return jnp.einsum('bhqk,bkhd->bqhd', p, v)

    out = lax.map(block, (jnp.arange(seq // Q_BLOCK), _to_blocks(q_nope), _to_blocks(q_pe)))
    return _from_blocks(out)


def _stick_breaking(z):
    bsz, seq, _ = z.shape
    q, k, v = [t.reshape(bsz, seq, SB_HEADS, SB_HEAD_DIM) for t in jnp.split(z, 3, axis=-1)]
    scale = SB_HEAD_DIM ** -0.5
    k_pos = jnp.arange(seq)

    def block(args):
        i, qb = args
        q_pos = i * Q_BLOCK + jnp.arange(Q_BLOCK)
        logits = jnp.einsum('bqhd,bkhd->bhqk', qb, k, preferred_element_type=jnp.float32) * scale
        mask = k_pos[None, :] < q_pos[:, None]
        log_1m = jnp.where(mask, jax.nn.log_sigmoid(-logits), 0.0)
        rev = lax.cumsum(log_1m, axis=3, reverse=True)
        between = jnp.concatenate([rev[..., 1:], jnp.zeros_like(rev[..., :1])], axis=-1)
        a = jnp.where(mask, jnp.exp(jax.nn.log_sigmoid(logits) + between), 0.0)
        return jnp.einsum('bhqk,bkhd->bqhd', a.astype(v.dtype), v)

    out = lax.map(block, (jnp.arange(seq // Q_BLOCK), _to_blocks(q)))
    return _from_blocks(out)


def _token_mixing(h, cos, sin, w_in, sg_norm_g, sg_w, sg_b, conv_w,
                  q_norm_g, w_uq, kv_norm_g, w_ukv, w_branch, w_out):
    bsz, seq, _ = h.shape
    z = h @ w_in
    z_sg, z_conv, z_mla, z_sb, z_gate = jnp.split(z, list(SPLIT_POINTS), axis=-1)
    ys = jnp.stack([
        _spatial_gating(z_sg, sg_norm_g, sg_w, sg_b),
        _short_conv(z_conv, conv_w),
        _latent_attention(z_mla, cos, sin, q_norm_g, w_uq, kv_norm_g, w_ukv),
        _stick_breaking(z_sb),
    ], axis=2)
    up = jnp.einsum('bsnw,nwd->bsnd', ys, w_branch)
    gates = jax.nn.sigmoid(z_gate).reshape(bsz, seq, N_BRANCH, D_MODEL)
    merged = jnp.sum(gates * up, axis=2)
    return merged @ w_out


def _squared_relu_mlp(h, w1, w2):
    return jnp.square(jax.nn.relu(h @ w1)) @ w2


def setup_inputs(seed: int = 0) -> dict:
    key = jax.random.key(seed)
    ks = jax.random.split(key, 21)
    f32 = jnp.float32

    def nrm(k, shape, scale):
        return jax.random.normal(k, shape, f32) * scale

    def gain(k, shape):
        return 1.0 + 0.02 * jax.random.normal(k, shape, f32)

    start = jax.random.randint(ks[2], (BATCH, 1), 0, 4096, dtype=jnp.int32)
    positions = start + jnp.arange(SEQ, dtype=jnp.int32)[None, :]
    return {
        'x': nrm(ks[0], (BATCH, SEQ, D_MODEL), 1.0),
        'c': nrm(ks[1], (BATCH, D_MODEL), 1.0),
        'positions': positions,
        'ada_w': nrm(ks[3], (DEPTH, D_MODEL, 6 * D_MODEL), 0.5 * D_MODEL ** -0.5),
        'ada_b': nrm(ks[4], (DEPTH, 6 * D_MODEL), 0.02),
        'norm1_g': gain(ks[5], (DEPTH, D_MODEL)),
        'norm2_g': gain(ks[6], (DEPTH, D_MODEL)),
        'w_in': nrm(ks[7], (DEPTH, D_MODEL, IN_COLS), D_MODEL ** -0.5),
        'sg_norm_g': gain(ks[8], (DEPTH, BRANCH_WIDTH)),
        'sg_w': nrm(ks[9], (DEPTH, SG_GROUPS, CHUNK, CHUNK), CHUNK ** -0.5),
        'sg_b': gain(ks[10], (DEPTH, SG_GROUPS, CHUNK)),
        'conv_w': nrm(ks[11], (DEPTH, CONV_WIDTH, BRANCH_WIDTH), CONV_WIDTH ** -0.5),
        'mla_q_norm_g': gain(ks[12], (DEPTH, MLA_Q_RANK)),
        'mla_w_uq': nrm(ks[13], (DEPTH, MLA_Q_RANK, MLA_HEADS * (MLA_NOPE_DIM + MLA_ROPE_DIM)), MLA_Q_RANK ** -0.5),
        'mla_kv_norm_g': gain(ks[14], (DEPTH, MLA_KV_RANK)),
        'mla_w_ukv': nrm(ks[15], (DEPTH, MLA_KV_RANK, MLA_HEADS * (MLA_NOPE_DIM + MLA_V_DIM)), MLA_KV_RANK ** -0.5),
        'w_branch': nrm(ks[16], (DEPTH, N_BRANCH, BRANCH_WIDTH, D_MODEL), BRANCH_WIDTH ** -0.5),
        'w_out': nrm(ks[17], (DEPTH, D_MODEL, D_MODEL), D_MODEL ** -0.5),
        'mlp_w1': nrm(ks[18], (DEPTH, D_MODEL, D_FF), D_MODEL ** -0.5),
        'mlp_w2': nrm(ks[19], (DEPTH, D_FF, D_MODEL), 0.5 * D_FF ** -0.5),
        'final_norm_g': gain(ks[20], (D_MODEL,)),
    }


def reference(x, c, positions, ada_w, ada_b, norm1_g, norm2_g, w_in, sg_norm_g, sg_w, sg_b,
              conv_w, mla_q_norm_g, mla_w_uq, mla_kv_norm_g, mla_w_ukv, w_branch, w_out,
              mlp_w1, mlp_w2, final_norm_g):
    inv_freq = ROPE_THETA ** (-jnp.arange(0, MLA_ROPE_DIM, 2, dtype=jnp.float32) / MLA_ROPE_DIM)
    ang = positions.astype(jnp.float32)[..., None] * inv_freq
    cos = jnp.cos(ang).astype(x.dtype)
    sin = jnp.sin(ang).astype(x.dtype)
    c_act = jax.nn.silu(c)
    for l in range(DEPTH):
        mod = c_act @ ada_w[l] + ada_b[l]
        sh1, sc1, g1, sh2, sc2, g2 = jnp.split(mod[:, None, :], 6, axis=-1)
        h = _rmsnorm(x, norm1_g[l]) * (1.0 + sc1) + sh1
        x = x + g1 * _token_mixing(h, cos, sin, w_in[l], sg_norm_g[l], sg_w[l], sg_b[l], conv_w[l],
                                   mla_q_norm_g[l], mla_w_uq[l], mla_kv_norm_g[l], mla_w_ukv[l],
                                   w_branch[l], w_out[l])
        h = _rmsnorm(x, norm2_g[l]) * (1.0 + sc2) + sh2
        x = x + g2 * _squared_relu_mlp(h, mlp_w1[l], mlp_w2[l])
    return _rmsnorm(x, final_norm_g)
```

```python
import functools

import jax
import jax.numpy as jnp
from jax import lax
from jax.experimental import pallas as pl
from jax.experimental.pallas import tpu as pltpu

F32 = jnp.float32
BF16 = jnp.bfloat16

N_BRANCH = 4
CHUNK = 128
SG_GROUPS = 4
CONV_WIDTH = 3
MLA_HEADS = 8
MLA_NOPE_DIM = 64
MLA_ROPE_DIM = 32
MLA_Q_RANK = 256
MLA_KV_RANK = 128
ROPE_THETA = 10000.0
SB_HEADS = 8
NORM_EPS = 1e-6

LANES = 128
HEAD_SLOT = LANES
VMEM_LIMIT = 56 * 1024 * 1024


def _dot(a, b):
    return jnp.dot(a, b, preferred_element_type=F32)


def _dot_nt(a, b):
    return lax.dot_general(a, b, (((1,), (1,)), ((), ())), preferred_element_type=F32)


def _const_spec(shape):
    return pl.BlockSpec(shape, lambda *_: (0,) * len(shape), pipeline_mode=pl.Buffered(1))


def _rms(x):
    return x * lax.rsqrt(jnp.mean(x * x, axis=-1, keepdims=True) + NORM_EPS)


def _mod_kernel(c_ref, w_ref, b_ref, o_ref):
    c = c_ref[...]
    c_act = c * jax.nn.sigmoid(c)
    o_ref[0] = jnp.dot(c_act, w_ref[0], preferred_element_type=F32,
                       precision=lax.Precision.HIGHEST) + b_ref[0]


def _modulation(c, ada_w, ada_b):
    depth, d, cols = ada_w.shape
    rows = 8
    c_pad = jnp.zeros((rows, d), F32).at[:c.shape[0]].set(c)
    tn = 1536
    out = pl.pallas_call(
        _mod_kernel,
        grid=(depth, cols // tn),
        in_specs=[pl.BlockSpec((rows, d), lambda l, j: (0, 0)),
                  pl.BlockSpec((1, d, tn), lambda l, j: (l, 0, j)),
                  pl.BlockSpec((1, 1, tn), lambda l, j: (l, 0, j))],
        out_specs=pl.BlockSpec((1, rows, tn), lambda l, j: (l, 0, j)),
        out_shape=jax.ShapeDtypeStruct((depth, rows, cols), F32),
        name="adaln_mod",
    )(c_pad, ada_w, ada_b.reshape(depth, 1, cols))
    return out[:, :c.shape[0]]


def _rope_kernel(pos_ref, invf_ref, cos_ref, sin_ref):
    ang = pos_ref[...].astype(F32) * invf_ref[...]
    lane = lax.broadcasted_iota(jnp.int32, ang.shape, 1)
    is_rope = jnp.logical_and(lane >= MLA_NOPE_DIM, lane < MLA_NOPE_DIM + MLA_ROPE_DIM)
    cos_ref[...] = jnp.where(lane < MLA_NOPE_DIM, 1.0, jnp.where(is_rope, jnp.cos(ang), 0.0))
    sin_ref[...] = jnp.where(is_rope, jnp.sin(ang), 0.0)


def _rope_tables(positions):
    n = positions.size
    half = MLA_ROPE_DIM // 2
    inv_freq = ROPE_THETA ** (-jnp.arange(0, MLA_ROPE_DIM, 2, dtype=F32) / MLA_ROPE_DIM)
    invf = jnp.zeros((1, HEAD_SLOT), F32)
    invf = invf.at[0, MLA_NOPE_DIM:MLA_NOPE_DIM + half].set(inv_freq)
    invf = invf.at[0, MLA_NOPE_DIM + half:MLA_NOPE_DIM + 2 * half].set(inv_freq)
    tm = min(1024, n)
    return pl.pallas_call(
        _rope_kernel,
        grid=(n // tm,),
        in_specs=[pl.BlockSpec((tm, 1), lambda i: (i, 0)),
                  pl.BlockSpec((1, HEAD_SLOT), lambda i: (0, 0))],
        out_specs=[pl.BlockSpec((tm, HEAD_SLOT), lambda i: (i, 0))] * 2,
        out_shape=[jax.ShapeDtypeStruct((n, HEAD_SLOT), F32)] * 2,
        name="rope_tables",
    )(positions.reshape(n, 1), invf)


def _premix_kernel(x_ref, sh_ref, sc_ref, n1g_ref, wsg_ref, wconv_ref, wmla_ref, wsb_ref, wgate_ref,
                   sgng_ref, sgw_ref, sgbt_ref, convw_ref, qng_ref, wqa_ref, wqb_ref, kvng_ref,
                   wk_ref, wv_ref, wb_ref, cos_ref, sin_ref,
                   part_ref, g23_ref, qm_ref, km_ref, vm_ref, qs_ref, ks_ref, vs_ref,
                   tbuf_ref, ysg_ref, *, tm, tiles_per_seq, width, mla_scale, sb_scale):
    i = pl.program_id(0)
    d_model = x_ref.shape[1]
    x = x_ref[...]
    h = _rms(x) * n1g_ref[...] * (1.0 + sc_ref[0]) + sh_ref[0]
    hb = h.astype(BF16)

    gz = jax.nn.gelu(_dot(hb, wsg_ref[...]), approximate=True)
    u = gz[:, :width]
    v = gz[:, width:]
    vc = v - jnp.mean(v, axis=-1, keepdims=True)
    vn = vc * lax.rsqrt(jnp.mean(vc * vc, axis=-1, keepdims=True) + NORM_EPS) * sgng_ref[...]
    vnb = vn.astype(BF16)
    row = lax.broadcasted_iota(jnp.int32, (CHUNK, CHUNK), 0)
    col = lax.broadcasted_iota(jnp.int32, (CHUNK, CHUNK), 1)
    gdim = width // SG_GROUPS
    for g in range(SG_GROUPS):
        wg = jnp.where(row >= col, sgw_ref[g], 0.0).astype(BF16)
        bias = sgbt_ref[:, g:g + 1]
        for ch in range(tm // CHUNK):
            rs = slice(ch * CHUNK, (ch + 1) * CHUNK)
            cs = slice(g * gdim, (g + 1) * gdim)
            s = _dot(wg, vnb[rs, cs]) + bias
            ysg_ref[rs, cs] = (u[rs, cs] * s).astype(BF16)

    zc = _dot(hb, wconv_ref[...])
    t = zc[:, width:2 * width] * zc[:, 2 * width:]

    @pl.when(i % tiles_per_seq == 0)
    def _():
        tbuf_ref[0:8, :] = jnp.zeros((8, width), F32)

    tbuf_ref[8:8 + tm, :] = t
    y = (convw_ref[0:1, :] * tbuf_ref[6:6 + tm, :] + convw_ref[1:2, :] * tbuf_ref[7:7 + tm, :]
         + convw_ref[2:3, :] * t)
    tbuf_ref[0:8, :] = tbuf_ref[tm:tm + 8, :]
    ycv = (zc[:, :width] * y).astype(BF16)

    g01 = jax.nn.sigmoid(_dot(hb, wgate_ref[:, :2 * d_model]))
    part_ref[...] = (g01[:, :d_model] * _dot(ysg_ref[...], wb_ref[0])
                     + g01[:, d_model:] * _dot(ycv, wb_ref[1]))
    g23_ref[...] = jax.nn.sigmoid(_dot(hb, wgate_ref[:, 2 * d_model:]))

    zm = _dot(hb, wmla_ref[...])
    cos = cos_ref[...]
    sin = sin_ref[...]
    o = MLA_Q_RANK + MLA_KV_RANK
    cqn = (_rms(zm[:, :MLA_Q_RANK]) * qng_ref[...]).astype(BF16)
    ckvn = (_rms(zm[:, MLA_Q_RANK:o]) * kvng_ref[...]).astype(BF16)
    qa = _dot(cqn, wqa_ref[...])
    qb = _dot(cqn, wqb_ref[...])
    kk = _dot(ckvn, wk_ref[...])
    kpe = zm[:, o:o + HEAD_SLOT] * cos + zm[:, o + HEAD_SLOT:] * sin
    for hd in range(MLA_HEADS):
        sl = slice(hd * HEAD_SLOT, (hd + 1) * HEAD_SLOT)
        qm_ref[:, sl] = ((qa[:, sl] * cos + qb[:, sl] * sin) * mla_scale).astype(BF16)
        km_ref[:, sl] = (kk[:, sl] + kpe).astype(BF16)
    vm_ref[...] = _dot(ckvn, wv_ref[...]).astype(BF16)

    zs = _dot(hb, wsb_ref[...])
    qs_ref[...] = (zs[:, :width] * sb_scale).astype(BF16)
    ks_ref[...] = zs[:, width:2 * width].astype(BF16)
    vs_ref[...] = zs[:, 2 * width:].astype(BF16)


def _premix(x2, sh, sc, lw, cos_t, sin_t, *, seq, tm):
    n, d = x2.shape
    width = d // 2
    tiles_per_seq = seq // tm
    row = lambda c: pl.BlockSpec((tm, c), lambda i: (i, 0))
    vec = pl.BlockSpec((1, 1, d), lambda i: (i // tiles_per_seq, 0, 0))
    consts = [lw["n1g"], lw["w_sg"], lw["w_conv"], lw["w_mla"], lw["w_sb"], lw["w_gate"], lw["sg_norm_g"],
              lw["sg_w"], lw["sg_bt"], lw["conv_w"], lw["q_norm_g"], lw["w_qa"], lw["w_qb"], lw["kv_norm_g"],
              lw["w_k"], lw["w_v"], lw["w_b01"]]
    kern = functools.partial(_premix_kernel, tm=tm, tiles_per_seq=tiles_per_seq, width=width,
                             mla_scale=float((MLA_NOPE_DIM + MLA_ROPE_DIM) ** -0.5),
                             sb_scale=float((width // SB_HEADS) ** -0.5))
    out_cols = [(d, F32), (2 * d, F32), (MLA_HEADS * HEAD_SLOT, BF16), (MLA_HEADS * HEAD_SLOT, BF16),
                (width, BF16), (width, BF16), (width, BF16), (width, BF16)]
    return pl.pallas_call(
        kern,
        grid=(n // tm,),
        in_specs=[row(d), vec, vec] + [_const_spec(a.shape) for a in consts] + [row(HEAD_SLOT), row(HEAD_SLOT)],
        out_specs=[row(c) for c, _ in out_cols],
        out_shape=[jax.ShapeDtypeStruct((n, c), dt) for c, dt in out_cols],
        scratch_shapes=[pltpu.VMEM((tm + 8, width), F32), pltpu.VMEM((tm, width), BF16)],
        compiler_params=pltpu.CompilerParams(dimension_semantics=("arbitrary",), vmem_limit_bytes=VMEM_LIMIT),
        name="premix",
    )(x2, sh, sc, *consts, cos_t, sin_t)


def _mla_kernel(q_ref, k_ref, v_ref, o_ref, m_ref, l_ref, acc_ref, *, t):
    qi = pl.program_id(2)
    half = LANES // 2
    for hh in range(2):
        m_ref[hh] = jnp.full((t, 1), -jnp.inf, F32)
        l_ref[hh] = jnp.zeros((t, 1), F32)
        acc_ref[hh] = jnp.zeros((t, LANES), F32)

    def tile(j, masked):
        start = pl.multiple_of(j * t, t)
        ks = k_ref[pl.ds(start, t), :]
        vs = v_ref[pl.ds(start, t), :]
        for hh in range(2):
            sl = slice(hh * HEAD_SLOT, (hh + 1) * HEAD_SLOT)
            s = _dot_nt(q_ref[:, sl], ks[:, sl])
            if masked:
                r = lax.broadcasted_iota(jnp.int32, (t, t), 0)
                c = lax.broadcasted_iota(jnp.int32, (t, t), 1)
                s = jnp.where(c <= r, s, -jnp.inf)
            m_prev = m_ref[hh]
            m_new = jnp.maximum(m_prev, jnp.max(s, axis=-1, keepdims=True))
            alpha = jnp.exp(m_prev - m_new)
            p = jnp.exp(s - m_new)
            l_ref[hh] = alpha * l_ref[hh] + jnp.sum(p, axis=-1, keepdims=True)
            acc_ref[hh] = alpha * acc_ref[hh] + _dot(p.astype(BF16), vs)
            m_ref[hh] = m_new

    def body(j, carry):
        tile(j, False)
        return carry

    lax.fori_loop(0, qi, body, 0)
    tile(qi, True)
    lane = lax.broadcasted_iota(jnp.int32, (t, LANES), 1)
    o_ref[...] = jnp.where(lane < half, acc_ref[0] / l_ref[0], acc_ref[1] / l_ref[1]).astype(BF16)


def _latent_attention(qm, km, vm, *, batch, seq, t):
    n = qm.shape[0]
    nq = seq // t
    pairs = MLA_HEADS // 2
    return pl.pallas_call(
        functools.partial(_mla_kernel, t=t),
        grid=(batch, pairs, nq),
        in_specs=[pl.BlockSpec((t, 2 * HEAD_SLOT), lambda b, p, q: (b * nq + q, p)),
                  pl.BlockSpec((seq, 2 * HEAD_SLOT), lambda b, p, q: (b, p)),
                  pl.BlockSpec((seq, LANES), lambda b, p, q: (b, p))],
        out_specs=pl.BlockSpec((t, LANES), lambda b, p, q: (b * nq + q, p)),
        out_shape=jax.ShapeDtypeStruct((n, vm.shape[1]), BF16),
        scratch_shapes=[pltpu.VMEM((2, t, 1), F32), pltpu.VMEM((2, t, 1), F32), pltpu.VMEM((2, t, LANES), F32)],
        compiler_params=pltpu.CompilerParams(dimension_semantics=("arbitrary",) * 3, vmem_limit_bytes=VMEM_LIMIT),
        name="latent_attn",
    )(qm, km, vm)


def _sb_kernel(q_ref, k_ref, v_ref, o_ref, carry_ref, acc_ref, *, t):
    qi = pl.program_id(2)
    half = LANES // 2
    lane = lax.broadcasted_iota(jnp.int32, (t, LANES), 1)
    q2 = q_ref[...]
    zero = jnp.zeros_like(q2)
    qh = (jnp.where(lane < half, q2, zero), jnp.where(lane >= half, q2, zero))
    r = lax.broadcasted_iota(jnp.int32, (t, t), 0)
    c = lax.broadcasted_iota(jnp.int32, (t, t), 1)
    suffix = jnp.where(r >= c, 1.0, 0.0).astype(BF16)
    for hh in range(2):
        carry_ref[hh] = jnp.zeros((t, 1), F32)
        acc_ref[hh] = jnp.zeros((t, LANES), F32)

    def tile(j, masked):
        start = pl.multiple_of(j * t, t)
        ks = k_ref[pl.ds(start, t), :]
        vs = v_ref[pl.ds(start, t), :]
        for hh in range(2):
            z = _dot_nt(qh[hh], ks)
            log_1m = -(jnp.maximum(z, 0.0) + jnp.log(1.0 + jnp.exp(-jnp.abs(z))))
            if masked:
                log_1m = jnp.where(c < r, log_1m, 0.0)
            hi = log_1m.astype(BF16)
            lo = (log_1m - hi.astype(F32)).astype(BF16)
            cs = _dot(hi, suffix) + _dot(lo, suffix)
            carry = carry_ref[hh]
            a = jnp.exp(z + carry + cs)
            if masked:
                a = jnp.where(c < r, a, 0.0)
            acc_ref[hh] = acc_ref[hh] + _dot(a.astype(BF16), vs)
            carry_ref[hh] = carry + cs[:, 0:1]

    def body(jj, carry):
        tile(qi - 1 - jj, False)
        return carry

    tile(qi, True)
    lax.fori_loop(0, qi, body, 0)
    o_ref[...] = jnp.where(lane < half, acc_ref[0], acc_ref[1]).astype(BF16)


def _stick_breaking(qs, ks, vs, *, batch, seq, t):
    n, width = qs.shape
    nq = seq // t
    pairs = width // LANES
    return pl.pallas_call(
        functools.partial(_sb_kernel, t=t),
        grid=(batch, pairs, nq),
        in_specs=[pl.BlockSpec((t, LANES), lambda b, p, q: (b * nq + q, p)),
                  pl.BlockSpec((seq, LANES), lambda b, p, q: (b, p)),
                  pl.BlockSpec((seq, LANES), lambda b, p, q: (b, p))],
        out_specs=pl.BlockSpec((t, LANES), lambda b, p, q: (b * nq + q, p)),
        out_shape=jax.ShapeDtypeStruct((n, width), BF16),
        scratch_shapes=[pltpu.VMEM((2, t, 1), F32), pltpu.VMEM((2, t, LANES), F32)],
        compiler_params=pltpu.CompilerParams(dimension_semantics=("arbitrary",) * 3, vmem_limit_bytes=VMEM_LIMIT),
        name="stick_attn",
    )(qs, ks, vs)


def _post_kernel(x_ref, part_ref, g23_ref, ym_ref, ys_ref, g1_ref, sh2_ref, sc2_ref, g2_ref, n2g_ref,
                 wb_ref, wout_ref, w1_ref, w2_ref, fng_ref, o_ref, *, ff_chunk, final):
    d_model = x_ref.shape[1]
    g23 = g23_ref[...]
    merged = (part_ref[...] + g23[:, :d_model] * _dot(ym_ref[...], wb_ref[0])
              + g23[:, d_model:] * _dot(ys_ref[...], wb_ref[1]))
    x1 = x_ref[...] + g1_ref[0] * _dot(merged.astype(BF16), wout_ref[...])
    h2 = (_rms(x1) * n2g_ref[...] * (1.0 + sc2_ref[0]) + sh2_ref[0]).astype(BF16)
    mlp = jnp.zeros_like(x1)
    for k in range(w1_ref.shape[1] // ff_chunk):
        ks = slice(k * ff_chunk, (k + 1) * ff_chunk)
        a = jnp.maximum(_dot(h2, w1_ref[:, ks]), 0.0)
        mlp = mlp + _dot((a * a).astype(BF16), w2_ref[ks, :])
    x2 = x1 + g2_ref[0] * mlp
    if final:
        x2 = _rms(x2) * fng_ref[...]
    o_ref[...] = x2


def _post(x2, part, g23, ym, ys, g1, sh2, sc2, g2, lw, fng, *, seq, tm, final):
    n, d = x2.shape
    tiles_per_seq = seq // tm
    row = lambda c: pl.BlockSpec((tm, c), lambda i: (i, 0))
    vec = pl.BlockSpec((1, 1, d), lambda i: (i // tiles_per_seq, 0, 0))
    consts = [lw["n2g"], lw["w_b23"], lw["w_out"], lw["w1"], lw["w2"], fng]
    return pl.pallas_call(
        functools.partial(_post_kernel, ff_chunk=1024, final=final),
        grid=(n // tm,),
        in_specs=[row(d), row(d), row(2 * d), row(ym.shape[1]), row(ys.shape[1]), vec, vec, vec, vec]
        + [_const_spec(a.shape) for a in consts],
        out_specs=row(d),
        out_shape=jax.ShapeDtypeStruct((n, d), F32),
        compiler_params=pltpu.CompilerParams(dimension_semantics=("arbitrary",), vmem_limit_bytes=VMEM_LIMIT),
        name="post_mlp",
    )(x2, part, g23, ym, ys, g1, sh2, sc2, g2, *consts)


def _layer_weights(l, norm1_g, norm2_g, w_in, sg_norm_g, sg_w, sg_b, conv_w, mla_q_norm_g, mla_w_uq,
                   mla_kv_norm_g, mla_w_ukv, w_branch, w_out, mlp_w1, mlp_w2):
    d = w_in.shape[1]
    width = d // 2
    half = MLA_ROPE_DIM // 2
    qk = MLA_NOPE_DIM + MLA_ROPE_DIM
    c0 = 2 * width
    c1 = c0 + 3 * width
    c2 = c1 + MLA_Q_RANK + MLA_KV_RANK + MLA_ROPE_DIM
    c3 = c2 + 3 * width
    wi = w_in[l]
    w_mla = wi[:, c1:c2]
    o = MLA_Q_RANK + MLA_KV_RANK
    kpe = w_mla[:, o:]
    pad = lambda a, lo, hi: jnp.pad(a, [(0, 0)] * (a.ndim - 1) + [(lo, hi)])
    kpe_a = pad(kpe, MLA_NOPE_DIM, HEAD_SLOT - qk)
    kpe_b = pad(jnp.concatenate([-kpe[:, half:], kpe[:, :half]], axis=1), MLA_NOPE_DIM, HEAD_SLOT - qk)
    wq = mla_w_uq[l].reshape(MLA_Q_RANK, MLA_HEADS, qk)
    wq_a = pad(wq, 0, HEAD_SLOT - qk)
    wq_b = pad(jnp.concatenate([-wq[..., MLA_NOPE_DIM + half:], wq[..., MLA_NOPE_DIM:MLA_NOPE_DIM + half]],
                               axis=-1), MLA_NOPE_DIM, HEAD_SLOT - qk)
    wkv = mla_w_ukv[l].reshape(MLA_KV_RANK, MLA_HEADS, -1)
    w_k = pad(wkv[..., :MLA_NOPE_DIM], 0, HEAD_SLOT - MLA_NOPE_DIM)
    w_v = wkv[..., MLA_NOPE_DIM:]
    flat = lambda a: a.reshape(a.shape[0], -1).astype(BF16)
    return {
        "n1g": norm1_g[l][None, :], "n2g": norm2_g[l][None, :],
        "w_sg": wi[:, :c0].astype(BF16), "w_conv": wi[:, c0:c1].astype(BF16),
        "w_mla": jnp.concatenate([w_mla[:, :o], kpe_a, kpe_b], axis=1).astype(BF16),
        "w_sb": wi[:, c2:c3].astype(BF16), "w_gate": wi[:, c3:].astype(BF16),
        "sg_norm_g": sg_norm_g[l][None, :], "sg_w": sg_w[l], "sg_bt": sg_b[l].T, "conv_w": conv_w[l],
        "q_norm_g": mla_q_norm_g[l][None, :], "w_qa": flat(wq_a), "w_qb": flat(wq_b),
        "kv_norm_g": mla_kv_norm_g[l][None, :], "w_k": flat(w_k), "w_v": flat(w_v),
        "w_b01": w_branch[l, :2].astype(BF16), "w_b23": w_branch[l, 2:].astype(BF16),
        "w_out": w_out[l].astype(BF16), "w1": mlp_w1[l].astype(BF16), "w2": mlp_w2[l].astype(BF16),
    }


def kernel(x, c, positions, ada_w, ada_b, norm1_g, norm2_g, w_in, sg_norm_g, sg_w, sg_b, conv_w, mla_q_norm_g, mla_w_uq, mla_kv_norm_g, mla_w_ukv, w_branch, w_out, mlp_w1, mlp_w2, final_norm_g):
    batch, seq, d = x.shape
    depth = w_in.shape[0]
    tm = min(256, seq)
    t_attn = min(256, seq)
    mod = _modulation(c, ada_w, ada_b).reshape(depth, batch, 6, 1, d)
    cos_t, sin_t = _rope_tables(positions)
    xs = x.reshape(batch * seq, d)
    fng = final_norm_g[None, :]
    for l in range(depth):
        lw = _layer_weights(l, norm1_g, norm2_g, w_in, sg_norm_g, sg_w, sg_b, conv_w, mla_q_norm_g, mla_w_uq,
                            mla_kv_norm_g, mla_w_ukv, w_branch, w_out, mlp_w1, mlp_w2)
        sh1, sc1, g1, sh2, sc2, g2 = (mod[l, :, k] for k in range(6))
        part, g23, qm, km, vm, qs, ks, vs = _premix(xs, sh1, sc1, lw, cos_t, sin_t, seq=seq, tm=tm)
        ym = _latent_attention(qm, km, vm, batch=batch, seq=seq, t=t_attn)
        ys = _stick_breaking(qs, ks, vs, batch=batch, seq=seq, t=t_attn)
        xs = _post(xs, part, g23, ym, ys, g1, sh2, sc2, g2, lw, fng, seq=seq, tm=tm, final=(l == depth - 1))
    return xs.reshape(batch, seq, d)
```
